```python
import jax, jax.numpy as jnp
from jax import lax
import numpy as np

D_MODEL = 2048
BATCH = 1
SEQ = 8192
DEPTH = 1

PLE_DIM = 256
CONV_A_WIDTH = 1024
CONV_A_K = 3
CONF_WIDTH = 1024
CONF_K = 31
D_FF = -(-8 * D_MODEL // (3 * 256)) * 256
EPS = 1e-6
LN_EPS = 1e-5

IN_SPLIT_SIZES = (
    CONV_A_WIDTH,
    CONV_A_WIDTH,
    CONV_A_WIDTH,
    CONF_WIDTH,
    CONF_WIDTH,
    D_MODEL,
    D_MODEL,
)
IN_COLS = sum(IN_SPLIT_SIZES)

kernel_name = "hybrid_gated_conv_conformer_block"


def rmsnorm(x, g):
    xf = x.astype(jnp.float32)
    y = xf * lax.rsqrt(jnp.mean(xf * xf, axis=-1, keepdims=True) + EPS)
    return (y * g.astype(jnp.float32)).astype(x.dtype)


def layernorm(x, g, b):
    xf = x.astype(jnp.float32)
    mu = jnp.mean(xf, axis=-1, keepdims=True)
    xc = xf - mu
    var = jnp.mean(xc * xc, axis=-1, keepdims=True)
    y = xc * lax.rsqrt(var + LN_EPS)
    return (y * g.astype(jnp.float32) + b.astype(jnp.float32)).astype(x.dtype)


def causal_depthwise_conv(u, w):
    k, c = w.shape
    return lax.conv_general_dilated(
        u, w[:, None, :].astype(u.dtype),
        window_strides=(1,), padding=[(k - 1, 0)],
        dimension_numbers=("NWC", "WIO", "NWC"),
        feature_group_count=c)


def setup_inputs(seed: int = 0) -> dict:
    key = jax.random.key(seed)
    ks = jax.random.split(key, 24)
    f32 = jnp.float32
    L = DEPTH

    def nrm(k, shape, scale):
        return jax.random.normal(k, shape, f32) * scale

    def gain(k, shape):
        return 1.0 + 0.02 * jax.random.normal(k, shape, f32)

    return {
        "x": jax.random.normal(ks[0], (BATCH, SEQ, D_MODEL), f32),
        "p": jax.random.normal(ks[1], (DEPTH, BATCH, SEQ, PLE_DIM), f32),
        "g_mix": gain(ks[2], (L, D_MODEL)),
        "w_in": nrm(ks[3], (L, D_MODEL, IN_COLS), D_MODEL ** -0.5),
        "conv_a_w": nrm(ks[4], (L, CONV_A_K, CONV_A_WIDTH), CONV_A_K ** -0.5),
        "w_out_a": nrm(ks[5], (L, CONV_A_WIDTH, D_MODEL), CONV_A_WIDTH ** -0.5),
        "b_glu": nrm(ks[6], (L, 2 * CONF_WIDTH), 0.02),
        "conf_dw_w": nrm(ks[7], (L, CONF_K, CONF_WIDTH), CONF_K ** -0.5),
        "conf_dw_b": nrm(ks[8], (L, CONF_WIDTH), 0.02),
        "conf_ln_g": gain(ks[9], (L, CONF_WIDTH)),
        "conf_ln_b": nrm(ks[10], (L, CONF_WIDTH), 0.02),
        "w_pw_b": nrm(ks[11], (L, CONF_WIDTH, D_MODEL), CONF_WIDTH ** -0.5),
        "b_pw_b": nrm(ks[12], (L, D_MODEL), 0.02),
        "w_o": nrm(ks[13], (L, D_MODEL, D_MODEL), D_MODEL ** -0.5),
        "g_ffn": gain(ks[14], (L, D_MODEL)),
        "w_gate": nrm(ks[15], (L, D_MODEL, D_FF), D_MODEL ** -0.5),
        "w_up": nrm(ks[16], (L, D_MODEL, D_FF), D_MODEL ** -0.5),
        "w_down": nrm(ks[17], (L, D_FF, D_MODEL), D_FF ** -0.5),
        "g_ple": gain(ks[18], (L, D_MODEL)),
        "w_ple_gate": nrm(ks[19], (L, D_MODEL, D_MODEL), D_MODEL ** -0.5),
        "w_ple_proj": nrm(ks[20], (L, PLE_DIM, D_MODEL), PLE_DIM ** -0.5),
        "g_final": gain(ks[21], (D_MODEL,)),
    }


def reference(x, p, g_mix, w_in, conv_a_w, w_out_a, b_glu, conf_dw_w, conf_dw_b,
              conf_ln_g, conf_ln_b, w_pw_b, b_pw_b, w_o, g_ffn, w_gate, w_up, w_down,
              g_ple, w_ple_gate, w_ple_proj, g_final):
    h = x
    cuts = np.cumsum(IN_SPLIT_SIZES)[:-1].tolist()
    for i in range(DEPTH):
        n = rmsnorm(h, g_mix[i])
        proj = jnp.einsum("bsd,dc->bsc", n, w_in[i])
        a_h, a_b, a_c, glu_v, glu_g, gate_a, gate_b = jnp.split(proj, cuts, axis=-1)

        y_a = a_b * causal_depthwise_conv(a_c * a_h, conv_a_w[i])
        y_a = jnp.einsum("bsc,cd->bsd", y_a, w_out_a[i])

        bv, bg = jnp.split(b_glu[i], 2)
        u = (glu_v + bv) * jax.nn.sigmoid(glu_g + bg)
        v = causal_depthwise_conv(u, conf_dw_w[i]) + conf_dw_b[i]
        v = jax.nn.silu(layernorm(v, conf_ln_g[i], conf_ln_b[i]))
        y_b = jnp.einsum("bsc,cd->bsd", v, w_pw_b[i]) + b_pw_b[i]

        m = jax.nn.sigmoid(gate_a) * y_a + jax.nn.sigmoid(gate_b) * y_b
        h = h + jnp.einsum("bsd,de->bse", m, w_o[i])

        n2 = rmsnorm(h, g_ffn[i])
        f = jax.nn.silu(jnp.einsum("bsd,df->bsf", n2, w_gate[i])) * jnp.einsum("bsd,df->bsf", n2, w_up[i])
        h = h + jnp.einsum("bsf,fd->bsd", f, w_down[i])

        n3 = rmsnorm(h, g_ple[i])
        ple = jnp.einsum("bse,ed->bsd", p[i].astype(h.dtype), w_ple_proj[i])
        h = h + jax.nn.sigmoid(jnp.einsum("bsd,de->bse", n3, w_ple_gate[i])) * ple

    return rmsnorm(h, g_final)
```

```python
import functools

import jax
import jax.numpy as jnp
from jax import lax
from jax.experimental import pallas as pl
from jax.experimental.pallas import tpu as pltpu

F32 = jnp.float32
BF16 = jnp.bfloat16

EPS = 1e-6
LN_EPS = 1e-5

SUBLANES = 8
LANES = 128
HALO = 32
VMEM_LIMIT_BYTES = 56 * 1024 * 1024


def _rmsnorm_rows(x, g):
    ms = jnp.mean(x * x, axis=-1, keepdims=True)
    return x * lax.rsqrt(ms + EPS) * g


def _sigmoid(x):
    return 1.0 / (1.0 + jnp.exp(-x))


def _dot(a, b):
    return jnp.dot(a, b, preferred_element_type=F32)


def _in_proj_kernel(x_ref, g_ref, wah_ref, wab_ref, wac_ref, wgv_ref, wgg_ref, wgt_ref,
                    bv_ref, bg_ref, ca_ref, ab_ref, u_ref, gt_ref, n_ref):
    @pl.when(pl.program_id(1) == 0)
    def _():
        n_ref[...] = _rmsnorm_rows(x_ref[...], g_ref[...]).astype(BF16)

    ca_ref[...] = _dot(n_ref[...], wac_ref[...]) * _dot(n_ref[...], wah_ref[...])
    ab_ref[...] = _dot(n_ref[...], wab_ref[...]).astype(BF16)
    u_ref[...] = (_dot(n_ref[...], wgv_ref[...]) + bv_ref[...]) * _sigmoid(
        _dot(n_ref[...], wgg_ref[...]) + bg_ref[...])
    gt_ref[...] = _sigmoid(_dot(n_ref[...], wgt_ref[...])).astype(BF16)


def _in_proj(x, g_mix, w_in, b_glu, *, tm, width, d_model):
    s = x.shape[0]
    nj = 4
    tn = width // nj
    tg = 2 * d_model // nj
    gate_col0 = 5 * width

    def wspec(col0):
        return pl.BlockSpec((d_model, tn), lambda i, j, c=col0 // tn: (0, c + j))

    return pl.pallas_call(
        _in_proj_kernel,
        grid=(s // tm, nj),
        in_specs=[
            pl.BlockSpec((tm, d_model), lambda i, j: (i, 0)),
            pl.BlockSpec((1, d_model), lambda i, j: (0, 0)),
            wspec(0), wspec(width), wspec(2 * width), wspec(3 * width), wspec(4 * width),
            pl.BlockSpec((d_model, tg), lambda i, j, c=gate_col0 // tg: (0, c + j)),
            pl.BlockSpec((1, tn), lambda i, j: (0, j)),
            pl.BlockSpec((1, tn), lambda i, j, c=width // tn: (0, c + j)),
        ],
        out_specs=[
            pl.BlockSpec((tm, tn), lambda i, j: (i, j)),
            pl.BlockSpec((tm, tn), lambda i, j: (i, j)),
            pl.BlockSpec((tm, tn), lambda i, j: (i, j)),
            pl.BlockSpec((tm, tg), lambda i, j: (i, j)),
        ],
        out_shape=[
            jax.ShapeDtypeStruct((s, width), F32),
            jax.ShapeDtypeStruct((s, width), BF16),
            jax.ShapeDtypeStruct((s, width), F32),
            jax.ShapeDtypeStruct((s, 2 * d_model), BF16),
        ],
        scratch_shapes=[pltpu.VMEM((tm, d_model), BF16)],
        compiler_params=pltpu.CompilerParams(
            dimension_semantics=("arbitrary", "arbitrary"),
            vmem_limit_bytes=VMEM_LIMIT_BYTES),
        name="in_proj",
    )(x, g_mix, w_in, w_in, w_in, w_in, w_in, w_in, b_glu, b_glu)


def _causal_dwconv(ext_ref, w_ref, out_ref, *, taps, tm, rc):
    width = ext_ref.shape[1]
    n_q = -(-taps // SUBLANES)

    def body(c, carry):
        r0 = pl.multiple_of(c * rc, rc)
        for lt in range(width // LANES):
            lanes = slice(lt * LANES, (lt + 1) * LANES)
            span = ext_ref[pl.ds(r0, rc + HALO), lanes]
            acc = None
            for res in range(min(SUBLANES, taps)):
                z = None
                for q in range(n_q):
                    delay = SUBLANES * q + res
                    if delay > taps - 1:
                        continue
                    k = taps - 1 - delay
                    lo = HALO - SUBLANES - SUBLANES * q
                    term = span[lo:lo + rc + SUBLANES] * w_ref[k:k + 1, lanes]
                    z = term if z is None else z + term
                if res:
                    z = pltpu.roll(z, res, axis=0)
                z = z[SUBLANES:SUBLANES + rc]
                acc = z if acc is None else acc + z
            out_ref[pl.ds(r0, rc), lanes] = acc
        return carry

    lax.fori_loop(0, tm // rc, body, 0)


def _mixers_kernel(x_ref, ca_ref, cah_ref, ab_ref, u_ref, uh_ref, gta_ref, gtb_ref,
                   caw_ref, dww_ref, dwb_ref, lng_ref, lnb_ref,
                   woa_ref, wpw_ref, bpw_ref, wo_ref,
                   h1_ref,
                   ext_ref, cv_ref, ya_ref, v_ref, m_ref, *, tm, nj, rc):
    i = pl.program_id(0)
    j = pl.program_id(1)
    tn = woa_ref.shape[1]

    @pl.when(j == 0)
    def _():
        keep = (i > 0).astype(F32)
        ext_ref[0:HALO, :] = cah_ref[...] * keep
        ext_ref[HALO:HALO + tm, :] = ca_ref[...]
        _causal_dwconv(ext_ref, caw_ref, cv_ref, taps=caw_ref.shape[0], tm=tm, rc=rc)
        ya_ref[...] = (ab_ref[...].astype(F32) * cv_ref[...]).astype(BF16)
        ext_ref[0:HALO, :] = uh_ref[...] * keep
        ext_ref[HALO:HALO + tm, :] = u_ref[...]
        _causal_dwconv(ext_ref, dww_ref, cv_ref, taps=dww_ref.shape[0], tm=tm, rc=rc)
        c = cv_ref[...] + dwb_ref[...]
        mu = jnp.mean(c, axis=-1, keepdims=True)
        xc = c - mu
        var = jnp.mean(xc * xc, axis=-1, keepdims=True)
        y = xc * lax.rsqrt(var + LN_EPS) * lng_ref[...] + lnb_ref[...]
        v_ref[...] = (y * _sigmoid(y)).astype(BF16)

    @pl.when(j < nj)
    def _():
        ya = _dot(ya_ref[...], woa_ref[...])
        yb = _dot(v_ref[...], wpw_ref[...]) + bpw_ref[...]
        m = gta_ref[...].astype(F32) * ya + gtb_ref[...].astype(F32) * yb
        m_ref[j] = m.astype(BF16)

    @pl.when(j >= nj)
    def _():
        acc = x_ref[...]
        for jj in range(nj):
            acc = acc + _dot(m_ref[jj], wo_ref[jj * tn:(jj + 1) * tn, :])
        h1_ref[...] = acc


def _mixers(x, ca, ab, u, gates, conv_a_w, conf_dw_w, conf_dw_b, ln_g, ln_b,
            w_out_a, w_pw_b, b_pw_b, w_o, *, tm, tn, rc, width, d_model):
    s = x.shape[0]
    nj = d_model // tn
    hb = tm // HALO

    def col2(i, j):
        return (i, jnp.maximum(j - nj, 0))

    def col1(i, j):
        return (i, jnp.minimum(j, nj - 1))

    def halo(i, j):
        return (jnp.maximum(i * hb - 1, 0), 0)

    row = lambda i, j: (i, 0)
    fixed = lambda i, j: (0, 0)
    kernel = functools.partial(_mixers_kernel, tm=tm, nj=nj, rc=rc)
    return pl.pallas_call(
        kernel,
        grid=(s // tm, 2 * nj),
        in_specs=[
            pl.BlockSpec((tm, tn), col2),
            pl.BlockSpec((tm, width), row),
            pl.BlockSpec((HALO, width), halo),
            pl.BlockSpec((tm, width), row),
            pl.BlockSpec((tm, width), row),
            pl.BlockSpec((HALO, width), halo),
            pl.BlockSpec((tm, tn), col1),
            pl.BlockSpec((tm, tn), lambda i, j: (i, nj + jnp.minimum(j, nj - 1))),
            pl.BlockSpec(conv_a_w.shape, fixed),
            pl.BlockSpec(conf_dw_w.shape, fixed),
            pl.BlockSpec((1, width), fixed),
            pl.BlockSpec((1, width), fixed),
            pl.BlockSpec((1, width), fixed),
            pl.BlockSpec((width, tn), lambda i, j: (0, jnp.minimum(j, nj - 1))),
            pl.BlockSpec((width, tn), lambda i, j: (0, jnp.minimum(j, nj - 1))),
            pl.BlockSpec((1, tn), lambda i, j: (0, jnp.minimum(j, nj - 1))),
            pl.BlockSpec((d_model, tn), lambda i, j: (0, jnp.maximum(j - nj, 0))),
        ],
        out_specs=pl.BlockSpec((tm, tn), col2),
        out_shape=jax.ShapeDtypeStruct((s, d_model), F32),
        scratch_shapes=[
            pltpu.VMEM((HALO + tm, width), F32),
            pltpu.VMEM((tm, width), F32),
            pltpu.VMEM((tm, width), BF16),
            pltpu.VMEM((tm, width), BF16),
            pltpu.VMEM((nj, tm, tn), BF16),
        ],
        compiler_params=pltpu.CompilerParams(
            dimension_semantics=("arbitrary", "arbitrary"),
            vmem_limit_bytes=VMEM_LIMIT_BYTES),
        name="mixers",
    )(x, ca, ca, ab, u, u, gates, gates, conv_a_w, conf_dw_w, conf_dw_b, ln_g, ln_b,
      w_out_a, w_pw_b, b_pw_b, w_o)


def _ffn_kernel(h_ref, g_ref, wg_ref, wu_ref, wd_ref, o_ref, n_ref, acc_ref):
    k = pl.program_id(1)

    @pl.when(k == 0)
    def _():
        n_ref[...] = _rmsnorm_rows(h_ref[...], g_ref[...]).astype(BF16)
        acc_ref[...] = jnp.zeros_like(acc_ref)

    gate = _dot(n_ref[...], wg_ref[...])
    up = _dot(n_ref[...], wu_ref[...])
    f = (gate * _sigmoid(gate) * up).astype(BF16)
    acc_ref[...] += _dot(f, wd_ref[...])

    @pl.when(k == pl.num_programs(1) - 1)
    def _():
        o_ref[...] = h_ref[...] + acc_ref[...]


def _ffn(h, g_ffn, w_gate, w_up, w_down, *, tm, tf):
    s, d_model = h.shape
    d_ff = w_gate.shape[1]
    return pl.pallas_call(
        _ffn_kernel,
        grid=(s // tm, d_ff // tf),
        in_specs=[
            pl.BlockSpec((tm, d_model), lambda i, k: (i, 0)),
            pl.BlockSpec((1, d_model), lambda i, k: (0, 0)),
            pl.BlockSpec((d_model, tf), lambda i, k: (0, k)),
            pl.BlockSpec((d_model, tf), lambda i, k: (0, k)),
            pl.BlockSpec((tf, d_model), lambda i, k: (k, 0)),
        ],
        out_specs=pl.BlockSpec((tm, d_model), lambda i, k: (i, 0)),
        out_shape=jax.ShapeDtypeStruct((s, d_model), F32),
        scratch_shapes=[pltpu.VMEM((tm, d_model), BF16), pltpu.VMEM((tm, d_model), F32)],
        compiler_params=pltpu.CompilerParams(
            dimension_semantics=("arbitrary", "arbitrary"),
            vmem_limit_bytes=VMEM_LIMIT_BYTES),
        name="ffn",
    )(h, g_ffn, w_gate, w_up, w_down)


def _ple_kernel(h_ref, hc_ref, p_ref, g_ref, wpg_ref, wpp_ref, gf_ref, o_ref, n_ref, h3_ref, *, nj):
    j = pl.program_id(1)
    tn = wpg_ref.shape[1]

    @pl.when(j == 0)
    def _():
        n_ref[...] = _rmsnorm_rows(h_ref[...], g_ref[...]).astype(BF16)

    gate = _sigmoid(_dot(n_ref[...], wpg_ref[...]))
    ple = _dot(p_ref[...].astype(BF16), wpp_ref[...])
    h3_ref[j] = hc_ref[...] + gate * ple

    @pl.when(j == nj - 1)
    def _():
        ssq = None
        for jj in range(nj):
            blk = h3_ref[jj]
            part = jnp.sum(blk * blk, axis=-1, keepdims=True)
            ssq = part if ssq is None else ssq + part
        scale = lax.rsqrt(ssq / (nj * tn) + EPS)
        for jj in range(nj):
            o_ref[:, jj * tn:(jj + 1) * tn] = h3_ref[jj] * scale * gf_ref[:, jj * tn:(jj + 1) * tn]


def _ple(h, p, g_ple, w_ple_gate, w_ple_proj, g_final, *, tm, tn):
    s, d_model = h.shape
    ple_dim = p.shape[1]
    nj = d_model // tn
    return pl.pallas_call(
        functools.partial(_ple_kernel, nj=nj),
        grid=(s // tm, nj),
        in_specs=[
            pl.BlockSpec((tm, d_model), lambda i, j: (i, 0)),
            pl.BlockSpec((tm, tn), lambda i, j: (i, j)),
            pl.BlockSpec((tm, ple_dim), lambda i, j: (i, 0)),
            pl.BlockSpec((1, d_model), lambda i, j: (0, 0)),
            pl.BlockSpec((d_model, tn), lambda i, j: (0, j)),
            pl.BlockSpec((ple_dim, tn), lambda i, j: (0, j)),
            pl.BlockSpec((1, d_model), lambda i, j: (0, 0)),
        ],
        out_specs=pl.BlockSpec((tm, d_model), lambda i, j: (i, 0)),
        out_shape=jax.ShapeDtypeStruct((s, d_model), F32),
        scratch_shapes=[pltpu.VMEM((tm, d_model), BF16), pltpu.VMEM((nj, tm, tn), F32)],
        compiler_params=pltpu.CompilerParams(
            dimension_semantics=("arbitrary", "arbitrary"),
            vmem_limit_bytes=VMEM_LIMIT_BYTES),
        name="ple_final",
    )(h, h, p, g_ple, w_ple_gate, w_ple_proj, g_final)


def kernel(x, p, g_mix, w_in, conv_a_w, w_out_a, b_glu, conf_dw_w, conf_dw_b, conf_ln_g, conf_ln_b,
           w_pw_b, b_pw_b, w_o, g_ffn, w_gate, w_up, w_down, g_ple, w_ple_gate, w_ple_proj, g_final):
    batch, seq, d_model = x.shape
    depth = w_in.shape[0]
    width = conv_a_w.shape[2]
    assert batch == 1 and conf_dw_w.shape[2] == width
    assert conv_a_w.shape[1] - 1 <= HALO and conf_dw_w.shape[1] - 1 <= HALO

    tm = 512
    row = lambda a: a.reshape(1, -1)
    h = x.reshape(seq, d_model)
    for l in range(depth):
        ca, ab, u, gates = _in_proj(h, row(g_mix[l]), w_in[l].astype(BF16), row(b_glu[l]),
                                    tm=tm, width=width, d_model=d_model)
        h = _mixers(h, ca, ab, u, gates, conv_a_w[l], conf_dw_w[l], row(conf_dw_b[l]),
                    row(conf_ln_g[l]), row(conf_ln_b[l]),
                    w_out_a[l].astype(BF16), w_pw_b[l].astype(BF16), row(b_pw_b[l]),
                    w_o[l].astype(BF16), tm=tm, tn=512, rc=64, width=width, d_model=d_model)
        h = _ffn(h, row(g_ffn[l]), w_gate[l].astype(BF16), w_up[l].astype(BF16),
                 w_down[l].astype(BF16), tm=tm, tf=512)
        assert depth == 1
        h = _ple(h, p[l, 0], row(g_ple[l]), w_ple_gate[l].astype(BF16), w_ple_proj[l].astype(BF16),
                 row(g_final), tm=tm, tn=512)
    return h.reshape(batch, seq, d_model)
```

```python
import functools

import jax
import jax.numpy as jnp
from jax import lax
from jax.experimental import pallas as pl
from jax.experimental.pallas import tpu as pltpu

F32 = jnp.float32
BF16 = jnp.bfloat16

EPS = 1e-6
LN_EPS = 1e-5

SUBLANES = 8
LANES = 128
HALO = 32
VMEM_LIMIT_BYTES = 56 * 1024 * 1024


def _rmsnorm_rows(x, g):
    ms = jnp.mean(x * x, axis=-1, keepdims=True)
    return x * lax.rsqrt(ms + EPS) * g


def _sigmoid(x):
    return 1.0 / (1.0 + jnp.exp(-x))


def _dot(a, b):
    return jnp.dot(a, b, preferred_element_type=F32)


def _in_proj_kernel(x_ref, g_ref, wah_ref, wab_ref, wac_ref, wgv_ref, wgg_ref, wgt_ref,
                    bv_ref, bg_ref, ca_ref, ab_ref, u_ref, gt_ref, n_ref):
    @pl.when(pl.program_id(1) == 0)
    def _():
        n_ref[...] = _rmsnorm_rows(x_ref[...], g_ref[...]).astype(BF16)

    ca_ref[...] = _dot(n_ref[...], wac_ref[...]) * _dot(n_ref[...], wah_ref[...])
    ab_ref[...] = _dot(n_ref[...], wab_ref[...]).astype(BF16)
    u_ref[...] = (_dot(n_ref[...], wgv_ref[...]) + bv_ref[...]) * _sigmoid(
        _dot(n_ref[...], wgg_ref[...]) + bg_ref[...])
    gt_ref[...] = _sigmoid(_dot(n_ref[...], wgt_ref[...])).astype(BF16)


def _in_proj(x, g_mix, w_in, b_glu, *, tm, width, d_model):
    s = x.shape[0]
    nj = 4
    tn = width // nj
    tg = 2 * d_model // nj
    gate_col0 = 5 * width

    def wspec(col0):
        return pl.BlockSpec((d_model, tn), lambda i, j, c=col0 // tn: (0, c + j))

    return pl.pallas_call(
        _in_proj_kernel,
        grid=(s // tm, nj),
        in_specs=[
            pl.BlockSpec((tm, d_model), lambda i, j: (i, 0)),
            pl.BlockSpec((1, d_model), lambda i, j: (0, 0)),
            wspec(0), wspec(width), wspec(2 * width), wspec(3 * width), wspec(4 * width),
            pl.BlockSpec((d_model, tg), lambda i, j, c=gate_col0 // tg: (0, c + j)),
            pl.BlockSpec((1, tn), lambda i, j: (0, j)),
            pl.BlockSpec((1, tn), lambda i, j, c=width // tn: (0, c + j)),
        ],
        out_specs=[
            pl.BlockSpec((tm, tn), lambda i, j: (i, j)),
            pl.BlockSpec((tm, tn), lambda i, j: (i, j)),
            pl.BlockSpec((tm, tn), lambda i, j: (i, j)),
            pl.BlockSpec((tm, tg), lambda i, j: (i, j)),
        ],
        out_shape=[
            jax.ShapeDtypeStruct((s, width), F32),
            jax.ShapeDtypeStruct((s, width), BF16),
            jax.ShapeDtypeStruct((s, width), F32),
            jax.ShapeDtypeStruct((s, 2 * d_model), BF16),
        ],
        scratch_shapes=[pltpu.VMEM((tm, d_model), BF16)],
        compiler_params=pltpu.CompilerParams(
            dimension_semantics=("arbitrary", "arbitrary"),
            vmem_limit_bytes=VMEM_LIMIT_BYTES),
        name="in_proj",
    )(x, g_mix, w_in, w_in, w_in, w_in, w_in, w_in, b_glu, b_glu)


def _causal_dwconv(ext_ref, w_ref, out_ref, *, taps, tm, rc):
    width = ext_ref.shape[1]
    n_q = -(-taps // SUBLANES)

    def body(c, carry):
        r0 = pl.multiple_of(c * rc, rc)
        for lt in range(width // LANES):
            lanes = slice(lt * LANES, (lt + 1) * LANES)
            span = ext_ref[pl.ds(r0, rc + HALO), lanes]
            acc = None
            for res in range(min(SUBLANES, taps)):
                z = None
                for q in range(n_q):
                    delay = SUBLANES * q + res
                    if delay > taps - 1:
                        continue
                    k = taps - 1 - delay
                    lo = HALO - SUBLANES - SUBLANES * q
                    term = span[lo:lo + rc + SUBLANES] * w_ref[k:k + 1, lanes]
                    z = term if z is None else z + term
                if res:
                    z = pltpu.roll(z, res, axis=0)
                z = z[SUBLANES:SUBLANES + rc]
                acc = z if acc is None else acc + z
            out_ref[pl.ds(r0, rc), lanes] = acc
        return carry

    lax.fori_loop(0, tm // rc, body, 0)


def _mixers_kernel(x_ref, ca_ref, cah_ref, ab_ref, u_ref, uh_ref, gta_ref, gtb_ref,
                   caw_ref, dww_ref, dwb_ref, lng_ref, lnb_ref,
                   woa_ref, wpw_ref, bpw_ref, wo_ref,
                   h1_ref,
                   ext_ref, cv_ref, ya_ref, v_ref, m_ref, *, tm, nj, rc):
    i = pl.program_id(0)
    j = pl.program_id(1)
    tn = woa_ref.shape[1]

    @pl.when(j == 0)
    def _():
        keep = (i > 0).astype(F32)
        ext_ref[0:HALO, :] = cah_ref[...] * keep
        ext_ref[HALO:HALO + tm, :] = ca_ref[...]
        _causal_dwconv(ext_ref, caw_ref, cv_ref, taps=caw_ref.shape[0], tm=tm, rc=rc)
        ya_ref[...] = (ab_ref[...].astype(F32) * cv_ref[...]).astype(BF16)
        ext_ref[0:HALO, :] = uh_ref[...] * keep
        ext_ref[HALO:HALO + tm, :] = u_ref[...]
        _causal_dwconv(ext_ref, dww_ref, cv_ref, taps=dww_ref.shape[0], tm=tm, rc=rc)
        c = cv_ref[...] + dwb_ref[...]
        mu = jnp.mean(c, axis=-1, keepdims=True)
        xc = c - mu
        var = jnp.mean(xc * xc, axis=-1, keepdims=True)
        y = xc * lax.rsqrt(var + LN_EPS) * lng_ref[...] + lnb_ref[...]
        v_ref[...] = (y * _sigmoid(y)).astype(BF16)

    @pl.when(j < nj)
    def _():
        ya = _dot(ya_ref[...], woa_ref[...])
        yb = _dot(v_ref[...], wpw_ref[...]) + bpw_ref[...]
        m = gta_ref[...].astype(F32) * ya + gtb_ref[...].astype(F32) * yb
        m_ref[j] = m.astype(BF16)

    @pl.when(j >= nj)
    def _():
        acc = x_ref[...]
        for jj in range(nj):
            acc = acc + _dot(m_ref[jj], wo_ref[jj * tn:(jj + 1) * tn, :])
        h1_ref[...] = acc


def _mixers(x, ca, ab, u, gates, conv_a_w, conf_dw_w, conf_dw_b, ln_g, ln_b,
            w_out_a, w_pw_b, b_pw_b, w_o, *, tm, tn, rc, width, d_model):
    s = x.shape[0]
    nj = d_model // tn
    hb = tm // HALO

    def col2(i, j):
        return (i, jnp.maximum(j - nj, 0))

    def col1(i, j):
        return (i, jnp.minimum(j, nj - 1))

    def halo(i, j):
        return (jnp.maximum(i * hb - 1, 0), 0)

    row = lambda i, j: (i, 0)
    fixed = lambda i, j: (0, 0)
    kernel = functools.partial(_mixers_kernel, tm=tm, nj=nj, rc=rc)
    return pl.pallas_call(
        kernel,
        grid=(s // tm, 2 * nj),
        in_specs=[
            pl.BlockSpec((tm, tn), col2),
            pl.BlockSpec((tm, width), row),
            pl.BlockSpec((HALO, width), halo),
            pl.BlockSpec((tm, width), row),
            pl.BlockSpec((tm, width), row),
            pl.BlockSpec((HALO, width), halo),
            pl.BlockSpec((tm, tn), col1),
            pl.BlockSpec((tm, tn), lambda i, j: (i, nj + jnp.minimum(j, nj - 1))),
            pl.BlockSpec(conv_a_w.shape, fixed),
            pl.BlockSpec(conf_dw_w.shape, fixed),
            pl.BlockSpec((1, width), fixed),
            pl.BlockSpec((1, width), fixed),
            pl.BlockSpec((1, width), fixed),
            pl.BlockSpec((width, tn), lambda i, j: (0, jnp.minimum(j, nj - 1))),
            pl.BlockSpec((width, tn), lambda i, j: (0, jnp.minimum(j, nj - 1))),
            pl.BlockSpec((1, tn), lambda i, j: (0, jnp.minimum(j, nj - 1))),
            pl.BlockSpec((d_model, tn), lambda i, j: (0, jnp.maximum(j - nj, 0))),
        ],
        out_specs=pl.BlockSpec((tm, tn), col2),
        out_shape=jax.ShapeDtypeStruct((s, d_model), F32),
        scratch_shapes=[
            pltpu.VMEM((HALO + tm, width), F32),
            pltpu.VMEM((tm, width), F32),
            pltpu.VMEM((tm, width), BF16),
            pltpu.VMEM((tm, width), BF16),
            pltpu.VMEM((nj, tm, tn), BF16),
        ],
        compiler_params=pltpu.CompilerParams(
            dimension_semantics=("arbitrary", "arbitrary"),
            vmem_limit_bytes=VMEM_LIMIT_BYTES),
        name="mixers",
    )(x, ca, ca, ab, u, u, gates, gates, conv_a_w, conf_dw_w, conf_dw_b, ln_g, ln_b,
      w_out_a, w_pw_b, b_pw_b, w_o)


def _ffn_kernel(h_ref, g_ref, wg_ref, wu_ref, wd_ref, o_ref, n_ref):
    @pl.when(pl.program_id(1) == 0)
    def _():
        h = h_ref[...]
        n_ref[...] = _rmsnorm_rows(h, g_ref[...]).astype(BF16)
        o_ref[...] = h

    gate = _dot(n_ref[...], wg_ref[...].astype(BF16))
    up = _dot(n_ref[...], wu_ref[...].astype(BF16))
    f = (gate * _sigmoid(gate) * up).astype(BF16)
    o_ref[...] += _dot(f, wd_ref[...].astype(BF16))


def _ffn(h, g_ffn, w_gate, w_up, w_down, *, tm, tf):
    s, d_model = h.shape
    d_ff = w_gate.shape[1]
    return pl.pallas_call(
        _ffn_kernel,
        grid=(s // tm, d_ff // tf),
        in_specs=[
            pl.BlockSpec((tm, d_model), lambda i, k: (i, 0)),
            pl.BlockSpec((1, d_model), lambda i, k: (0, 0)),
            pl.BlockSpec((d_model, tf), lambda i, k: (0, k)),
            pl.BlockSpec((d_model, tf), lambda i, k: (0, k)),
            pl.BlockSpec((tf, d_model), lambda i, k: (k, 0)),
        ],
        out_specs=pl.BlockSpec((tm, d_model), lambda i, k: (i, 0)),
        out_shape=jax.ShapeDtypeStruct((s, d_model), F32),
        scratch_shapes=[pltpu.VMEM((tm, d_model), BF16)],
        compiler_params=pltpu.CompilerParams(
            dimension_semantics=("arbitrary", "arbitrary"),
            vmem_limit_bytes=VMEM_LIMIT_BYTES),
        name="ffn",
    )(h, g_ffn, w_gate, w_up, w_down)


def _ple_kernel(h_ref, hc_ref, p_ref, g_ref, wpg_ref, wpp_ref, gf_ref, o_ref, n_ref, h3_ref, *, nj):
    j = pl.program_id(1)
    tn = wpg_ref.shape[1]

    @pl.when(j == 0)
    def _():
        n_ref[...] = _rmsnorm_rows(h_ref[...], g_ref[...]).astype(BF16)

    gate = _sigmoid(_dot(n_ref[...], wpg_ref[...].astype(BF16)))
    ple = _dot(p_ref[...].astype(BF16), wpp_ref[...].astype(BF16))
    h3_ref[j] = hc_ref[...] + gate * ple

    @pl.when(j == nj - 1)
    def _():
        ssq = None
        for jj in range(nj):
            blk = h3_ref[jj]
            part = jnp.sum(blk * blk, axis=-1, keepdims=True)
            ssq = part if ssq is None else ssq + part
        scale = lax.rsqrt(ssq / (nj * tn) + EPS)
        for jj in range(nj):
            o_ref[:, jj * tn:(jj + 1) * tn] = h3_ref[jj] * scale * gf_ref[:, jj * tn:(jj + 1) * tn]


def _ple(h, p, g_ple, w_ple_gate, w_ple_proj, g_final, *, tm, tn):
    s, d_model = h.shape
    ple_dim = p.shape[1]
    nj = d_model // tn
    return pl.pallas_call(
        functools.partial(_ple_kernel, nj=nj),
        grid=(s // tm, nj),
        in_specs=[
            pl.BlockSpec((tm, d_model), lambda i, j: (i, 0)),
            pl.BlockSpec((tm, tn), lambda i, j: (i, j)),
            pl.BlockSpec((tm, ple_dim), lambda i, j: (i, 0)),
            pl.BlockSpec((1, d_model), lambda i, j: (0, 0)),
            pl.BlockSpec((d_model, tn), lambda i, j: (0, j)),
            pl.BlockSpec((ple_dim, tn), lambda i, j: (0, j)),
            pl.BlockSpec((1, d_model), lambda i, j: (0, 0)),
        ],
        out_specs=pl.BlockSpec((tm, d_model), lambda i, j: (i, 0)),
        out_shape=jax.ShapeDtypeStruct((s, d_model), F32),
        scratch_shapes=[pltpu.VMEM((tm, d_model), BF16), pltpu.VMEM((nj, tm, tn), F32)],
        compiler_params=pltpu.CompilerParams(
            dimension_semantics=("arbitrary", "arbitrary"),
            vmem_limit_bytes=VMEM_LIMIT_BYTES),
        name="ple_final",
    )(h, h, p, g_ple, w_ple_gate, w_ple_proj, g_final)


def kernel(x, p, g_mix, w_in, conv_a_w, w_out_a, b_glu, conf_dw_w, conf_dw_b, conf_ln_g, conf_ln_b,
           w_pw_b, b_pw_b, w_o, g_ffn, w_gate, w_up, w_down, g_ple, w_ple_gate, w_ple_proj, g_final):
    batch, seq, d_model = x.shape
    depth = w_in.shape[0]
    width = conv_a_w.shape[2]
    assert batch == 1 and conf_dw_w.shape[2] == width
    assert conv_a_w.shape[1] - 1 <= HALO and conf_dw_w.shape[1] - 1 <= HALO

    tm = 512
    row = lambda a: a.reshape(1, -1)
    h = x.reshape(seq, d_model)
    for l in range(depth):
        ca, ab, u, gates = _in_proj(h, row(g_mix[l]), w_in[l].astype(BF16), row(b_glu[l]),
                                    tm=tm, width=width, d_model=d_model)
        h = _mixers(h, ca, ab, u, gates, conv_a_w[l], conf_dw_w[l], row(conf_dw_b[l]),
                    row(conf_ln_g[l]), row(conf_ln_b[l]),
                    w_out_a[l].astype(BF16), w_pw_b[l].astype(BF16), row(b_pw_b[l]),
                    w_o[l].astype(BF16), tm=tm, tn=512, rc=64, width=width, d_model=d_model)
        h = _ffn(h, row(g_ffn[l]), w_gate[l], w_up[l], w_down[l], tm=1024, tf=256)
        assert depth == 1
        h = _ple(h, p[l, 0], row(g_ple[l]), w_ple_gate[l], w_ple_proj[l], row(g_final), tm=tm, tn=512)
    return h.reshape(batch, seq, d_model)
```

```python
import functools

import jax
import jax.numpy as jnp
from jax import lax
from jax.experimental import pallas as pl
from jax.experimental.pallas import tpu as pltpu

F32 = jnp.float32
BF16 = jnp.bfloat16

EPS = 1e-6
LN_EPS = 1e-5

SUBLANES = 8
LANES = 128
HALO = 32
CONV_ROWS = 64
VMEM_LIMIT_BYTES = 56 * 1024 * 1024


def _rmsnorm_rows(x, g):
    ms = jnp.mean(x * x, axis=-1, keepdims=True)
    return x * lax.rsqrt(ms + EPS) * g


def _sigmoid(x):
    return 1.0 / (1.0 + jnp.exp(-x))


def _dot(a, b):
    return jnp.dot(a, b, preferred_element_type=F32)


def _params(n_axes):
    return pltpu.CompilerParams(dimension_semantics=("arbitrary",) * n_axes,
                                vmem_limit_bytes=VMEM_LIMIT_BYTES)


def _dwconv_chunk(span, w_ref, lanes, taps):
    rc = CONV_ROWS
    acc = None
    for res in range(min(SUBLANES, taps)):
        z = None
        for q in range(-(-taps // SUBLANES)):
            delay = SUBLANES * q + res
            if delay > taps - 1:
                continue
            lo = HALO - SUBLANES - SUBLANES * q
            term = span[lo:lo + rc + SUBLANES] * w_ref[taps - 1 - delay:taps - delay, lanes]
            z = term if z is None else z + term
        if res:
            z = pltpu.roll(z, res, axis=0)
        z = z[SUBLANES:SUBLANES + rc]
        acc = z if acc is None else acc + z
    return acc


def _dwconv_tile_work(hist_ref, cur_ref, w_ref, emit):
    tm, tn = cur_ref.shape
    taps = w_ref.shape[0]

    def chunk(r0, lanes):
        if r0 == 0:
            span = jnp.concatenate([hist_ref[:, lanes], cur_ref[0:CONV_ROWS, lanes]], axis=0)
        else:
            span = cur_ref[r0 - HALO:r0 + CONV_ROWS, lanes]
        emit(slice(r0, r0 + CONV_ROWS), lanes, _dwconv_chunk(span, w_ref, lanes, taps))

    def refresh_history():
        hist_ref[...] = cur_ref[tm - HALO:tm, :]

    work = [functools.partial(chunk, r0, slice(l0, l0 + LANES))
            for r0 in range(0, tm, CONV_ROWS) for l0 in range(0, tn, LANES)]
    return work + [refresh_history]


def _interleave(major, minor):
    done = 0
    for k, item in enumerate(major):
        upto = len(minor) * (k + 1) // len(major)
        for m in minor[done:upto]:
            m()
        done = upto
        item()


def _in_proj_kernel(x_ref, g_ref, wah_ref, wab_ref, wac_ref, wgv_ref, wgg_ref, wgt_ref,
                    bv_ref, bg_ref, caw_ref, dww_ref, dwb_ref,
                    ya_ref, c_ref, gt_ref,
                    n_ref, hca_ref, hu_ref, ca0_ref, ca1_ref, u0_ref, u1_ref, ab0_ref, ab1_ref, *, nj):
    t = pl.program_id(0)
    last = pl.num_programs(0) - 2
    j = jnp.minimum(t, last) % nj
    jp = (t + nj - 1) % nj

    @pl.when(t == 0)
    def _():
        for ref in (hca_ref, hu_ref, ca1_ref, u1_ref, ab1_ref):
            ref[...] = jnp.zeros(ref.shape, ref.dtype)

    @pl.when(j == 0)
    def _():
        n_ref[...] = _rmsnorm_rows(x_ref[...], g_ref[...]).astype(BF16)

    def step(ca_w, u_w, ab_w, ca_r, u_r, ab_r):
        def emit_c(rows, lanes, acc):
            c_ref[rows, lanes] = acc + dwb_ref[:, lanes]

        def emit_ya(rows, lanes, acc):
            ya_ref[rows, lanes] = (ab_r[rows, lanes].astype(F32) * acc).astype(BF16)

        conv_b = _dwconv_tile_work(hu_ref.at[jp], u_r, dww_ref, emit_c)
        conv_a = _dwconv_tile_work(hca_ref.at[jp], ca_r, caw_ref, emit_ya)
        convs = [w for pair in zip(conv_b, conv_a) for w in pair]

        def glu_value():
            u_w[...] = _dot(n_ref[...], wgv_ref[...]) + bv_ref[...]

        def glu_gate():
            u_w[...] = u_w[...] * _sigmoid(_dot(n_ref[...], wgg_ref[...]) + bg_ref[...])

        def conv_a_value():
            ca_w[...] = _dot(n_ref[...], wah_ref[...])

        def conv_a_in_gate():
            ca_w[...] = ca_w[...] * _dot(n_ref[...], wac_ref[...])

        def conv_a_out_gate():
            ab_w[...] = _dot(n_ref[...], wab_ref[...]).astype(BF16)

        def merge_gates(c0):
            cols = slice(c0, c0 + tn)
            gt_ref[:, cols] = _sigmoid(_dot(n_ref[...], wgt_ref[:, cols])).astype(BF16)

        tn = u_w.shape[1]
        projections = [glu_value, glu_gate, conv_a_value, conv_a_in_gate, conv_a_out_gate] + [
            functools.partial(merge_gates, c0) for c0 in range(0, gt_ref.shape[1], tn)]
        _interleave(projections, convs)

    @pl.when(t % 2 == 0)
    def _():
        step(ca0_ref, u0_ref, ab0_ref, ca1_ref, u1_ref, ab1_ref)

    @pl.when(t % 2 == 1)
    def _():
        step(ca1_ref, u1_ref, ab1_ref, ca0_ref, u0_ref, ab0_ref)


def _in_proj(x, g_mix, w_in, b_glu, conv_a_w, conf_dw_w, conf_dw_b, *, tm, width, d_model):
    s = x.shape[0]
    nj = 4
    tn = width // nj
    tg = 2 * d_model // nj
    gate_col0 = 5 * width
    n_steps = (s // tm) * nj
    assert n_steps % 2 == 0

    cur = lambda t: jnp.minimum(t, n_steps - 1)
    prev = lambda t: jnp.maximum(t - 1, 0)

    def wspec(col0):
        return pl.BlockSpec((d_model, tn), lambda t, c=col0 // tn: (0, c + cur(t) % nj))

    prev_col = lambda t: (0, prev(t) % nj)
    prev_tile = lambda t: (prev(t) // nj, prev(t) % nj)
    return pl.pallas_call(
        functools.partial(_in_proj_kernel, nj=nj),
        grid=(n_steps + 1,),
        in_specs=[
            pl.BlockSpec((tm, d_model), lambda t: (cur(t) // nj, 0)),
            pl.BlockSpec((1, d_model), lambda t: (0, 0)),
            wspec(0), wspec(width), wspec(2 * width), wspec(3 * width), wspec(4 * width),
            pl.BlockSpec((d_model, tg), lambda t, c=gate_col0 // tg: (0, c + cur(t) % nj)),
            pl.BlockSpec((1, tn), lambda t: (0, cur(t) % nj)),
            pl.BlockSpec((1, tn), lambda t, c=width // tn: (0, c + cur(t) % nj)),
            pl.BlockSpec((conv_a_w.shape[0], tn), prev_col),
            pl.BlockSpec((conf_dw_w.shape[0], tn), prev_col),
            pl.BlockSpec((1, tn), prev_col),
        ],
        out_specs=[
            pl.BlockSpec((tm, tn), prev_tile),
            pl.BlockSpec((tm, tn), prev_tile),
            pl.BlockSpec((tm, tg), lambda t: (cur(t) // nj, cur(t) % nj)),
        ],
        out_shape=[
            jax.ShapeDtypeStruct((s, width), BF16),
            jax.ShapeDtypeStruct((s, width), F32),
            jax.ShapeDtypeStruct((s, 2 * d_model), BF16),
        ],
        scratch_shapes=[
            pltpu.VMEM((tm, d_model), BF16),
            pltpu.VMEM((nj, HALO, tn), F32),
            pltpu.VMEM((nj, HALO, tn), F32),
            pltpu.VMEM((tm, tn), F32), pltpu.VMEM((tm, tn), F32),
            pltpu.VMEM((tm, tn), F32), pltpu.VMEM((tm, tn), F32),
            pltpu.VMEM((tm, tn), BF16), pltpu.VMEM((tm, tn), BF16),
        ],
        compiler_params=_params(1),
        name="in_proj",
    )(x, g_mix, w_in, w_in, w_in, w_in, w_in, w_in, b_glu, b_glu, conv_a_w, conf_dw_w, conf_dw_b)


def _mixers_kernel(x_ref, ya_ref, c_ref, gta_ref, gtb_ref, lng_ref, lnb_ref,
                   woa_ref, wpw_ref, bpw_ref, wo_ref, h1_ref, v_ref, m_ref, *, nj):
    j = pl.program_id(1)
    tn = woa_ref.shape[1]

    @pl.when(j == 0)
    def _():
        c = c_ref[...]
        mu = jnp.mean(c, axis=-1, keepdims=True)
        xc = c - mu
        var = jnp.mean(xc * xc, axis=-1, keepdims=True)
        y = xc * lax.rsqrt(var + LN_EPS) * lng_ref[...] + lnb_ref[...]
        v_ref[...] = (y * _sigmoid(y)).astype(BF16)

    @pl.when(j < nj)
    def _():
        ya = _dot(ya_ref[...], woa_ref[...])
        yb = _dot(v_ref[...], wpw_ref[...]) + bpw_ref[...]
        m = gta_ref[...].astype(F32) * ya + gtb_ref[...].astype(F32) * yb
        m_ref[j] = m.astype(BF16)

    @pl.when(j >= nj)
    def _():
        acc = x_ref[...]
        for jj in range(nj):
            acc = acc + _dot(m_ref[jj], wo_ref[jj * tn:(jj + 1) * tn, :])
        h1_ref[...] = acc


def _mixers(x, ya, c, gates, ln_g, ln_b, w_out_a, w_pw_b, b_pw_b, w_o, *, tm, tn, width, d_model):
    s = x.shape[0]
    nj = d_model // tn

    first = lambda j: jnp.minimum(j, nj - 1)
    second = lambda j: jnp.maximum(j - nj, 0)
    row = lambda i, j: (i, 0)
    fixed = lambda i, j: (0, 0)
    return pl.pallas_call(
        functools.partial(_mixers_kernel, nj=nj),
        grid=(s // tm, 2 * nj),
        in_specs=[
            pl.BlockSpec((tm, tn), lambda i, j: (i, second(j))),
            pl.BlockSpec((tm, width), row),
            pl.BlockSpec((tm, width), row),
            pl.BlockSpec((tm, tn), lambda i, j: (i, first(j))),
            pl.BlockSpec((tm, tn), lambda i, j: (i, nj + first(j))),
            pl.BlockSpec((1, width), fixed),
            pl.BlockSpec((1, width), fixed),
            pl.BlockSpec((width, tn), lambda i, j: (0, first(j))),
            pl.BlockSpec((width, tn), lambda i, j: (0, first(j))),
            pl.BlockSpec((1, tn), lambda i, j: (0, first(j))),
            pl.BlockSpec((d_model, tn), lambda i, j: (0, second(j))),
        ],
        out_specs=pl.BlockSpec((tm, tn), lambda i, j: (i, second(j))),
        out_shape=jax.ShapeDtypeStruct((s, d_model), F32),
        scratch_shapes=[
            pltpu.VMEM((tm, width), BF16),
            pltpu.VMEM((nj, tm, tn), BF16),
        ],
        compiler_params=_params(2),
        name="mixers",
    )(x, ya, c, gates, gates, ln_g, ln_b, w_out_a, w_pw_b, b_pw_b, w_o)


def _ffn_kernel(h_ref, g_ref, wg_ref, wu_ref, wd_ref, o_ref, n_ref):
    @pl.when(pl.program_id(1) == 0)
    def _():
        h = h_ref[...]
        n_ref[...] = _rmsnorm_rows(h, g_ref[...]).astype(BF16)
        o_ref[...] = h

    gate = _dot(n_ref[...], wg_ref[...].astype(BF16))
    up = _dot(n_ref[...], wu_ref[...].astype(BF16))
    f = (gate * _sigmoid(gate) * up).astype(BF16)
    o_ref[...] += _dot(f, wd_ref[...].astype(BF16))


def _ffn(h, g_ffn, w_gate, w_up, w_down, *, tm, tf):
    s, d_model = h.shape
    d_ff = w_gate.shape[1]
    return pl.pallas_call(
        _ffn_kernel,
        grid=(s // tm, d_ff // tf),
        in_specs=[
            pl.BlockSpec((tm, d_model), lambda i, k: (i, 0)),
            pl.BlockSpec((1, d_model), lambda i, k: (0, 0)),
            pl.BlockSpec((d_model, tf), lambda i, k: (0, k)),
            pl.BlockSpec((d_model, tf), lambda i, k: (0, k)),
            pl.BlockSpec((tf, d_model), lambda i, k: (k, 0)),
        ],
        out_specs=pl.BlockSpec((tm, d_model), lambda i, k: (i, 0)),
        out_shape=jax.ShapeDtypeStruct((s, d_model), F32),
        scratch_shapes=[pltpu.VMEM((tm, d_model), BF16)],
        compiler_params=_params(2),
        name="ffn",
    )(h, g_ffn, w_gate, w_up, w_down)


def _ple_kernel(h_ref, p_ref, g_ref, wpg_ref, wpp_ref, gf_ref, o_ref, n_ref, *, tn):
    d_model = h_ref.shape[1]
    n_ref[...] = _rmsnorm_rows(h_ref[...], g_ref[...]).astype(BF16)
    pb = p_ref[...].astype(BF16)
    ssq = None
    for c0 in range(0, d_model, tn):
        cols = slice(c0, c0 + tn)
        gate = _sigmoid(_dot(n_ref[...], wpg_ref[:, cols].astype(BF16)))
        h3 = h_ref[:, cols] + gate * _dot(pb, wpp_ref[:, cols].astype(BF16))
        o_ref[:, cols] = h3
        part = jnp.sum(h3 * h3, axis=-1, keepdims=True)
        ssq = part if ssq is None else ssq + part
    scale = lax.rsqrt(ssq / d_model + EPS)
    for c0 in range(0, d_model, tn):
        cols = slice(c0, c0 + tn)
        o_ref[:, cols] = o_ref[:, cols] * scale * gf_ref[:, cols]


def _ple(h, p, g_ple, w_ple_gate, w_ple_proj, g_final, *, tm, tn):
    s, d_model = h.shape
    ple_dim = p.shape[1]
    fixed = lambda i: (0, 0)
    resident = dict(pipeline_mode=pl.Buffered(1))
    return pl.pallas_call(
        functools.partial(_ple_kernel, tn=tn),
        grid=(s // tm,),
        in_specs=[
            pl.BlockSpec((tm, d_model), lambda i: (i, 0)),
            pl.BlockSpec((tm, ple_dim), lambda i: (i, 0)),
            pl.BlockSpec((1, d_model), fixed),
            pl.BlockSpec((d_model, d_model), fixed, **resident),
            pl.BlockSpec((ple_dim, d_model), fixed, **resident),
            pl.BlockSpec((1, d_model), fixed),
        ],
        out_specs=pl.BlockSpec((tm, d_model), lambda i: (i, 0)),
        out_shape=jax.ShapeDtypeStruct((s, d_model), F32),
        scratch_shapes=[pltpu.VMEM((tm, d_model), BF16)],
        compiler_params=_params(1),
        name="ple_final",
    )(h, p, g_ple, w_ple_gate, w_ple_proj, g_final)


def kernel(x, p, g_mix, w_in, conv_a_w, w_out_a, b_glu, conf_dw_w, conf_dw_b, conf_ln_g, conf_ln_b,
           w_pw_b, b_pw_b, w_o, g_ffn, w_gate, w_up, w_down, g_ple, w_ple_gate, w_ple_proj, g_final):
    batch, seq, d_model = x.shape
    depth = w_in.shape[0]
    width = conv_a_w.shape[2]
    assert batch == 1 and depth == 1 and conf_dw_w.shape[2] == width
    assert conv_a_w.shape[1] - 1 <= HALO and conf_dw_w.shape[1] - 1 <= HALO

    row = lambda a: a.reshape(1, -1)
    h = x.reshape(seq, d_model)
    ya, c, gates = _in_proj(h, row(g_mix[0]), w_in[0].astype(BF16), row(b_glu[0]),
                            conv_a_w[0], conf_dw_w[0], row(conf_dw_b[0]),
                            tm=512, width=width, d_model=d_model)
    h = _mixers(h, ya, c, gates, row(conf_ln_g[0]), row(conf_ln_b[0]),
                w_out_a[0].astype(BF16), w_pw_b[0].astype(BF16), row(b_pw_b[0]), w_o[0].astype(BF16),
                tm=512, tn=512, width=width, d_model=d_model)
    h = _ffn(h, row(g_ffn[0]), w_gate[0], w_up[0], w_down[0], tm=1024, tf=256)
    h = _ple(h, p[0, 0], row(g_ple[0]), w_ple_gate[0], w_ple_proj[0], row(g_final), tm=512, tn=512)
    return h.reshape(batch, seq, d_model)
```

```python
import functools
from typing import NamedTuple

import jax
import jax.numpy as jnp
from jax import lax
from jax.experimental import pallas as pl
from jax.experimental.pallas import tpu as pltpu

F32 = jnp.float32
BF16 = jnp.bfloat16

EPS = 1e-6
LN_EPS = 1e-5

SUBLANES = 8
LANES = 128
MXU_COLS = 256
HALO = 32
CONV_ROWS = 64
V7X_VMEM_BYTES = 64 * 1024 * 1024
VMEM_LIMIT_BYTES = V7X_VMEM_BYTES * 7 // 8


class _Tiles(NamedTuple):
    in_proj_tm: int = 512
    mixers_tm: int = 512
    mixers_tn: int = 512
    ffn_tm: int = 1024
    ffn_tf: int = 512
    ple_tm: int = 512
    ple_tn: int = 256


TILES = _Tiles()


def _rmsnorm_rows(x, g):
    ms = jnp.mean(x * x, axis=-1, keepdims=True)
    return x * lax.rsqrt(ms + EPS) * g


NEG_LOG2_E = -1.4426950408889634


def _sigmoid(x):
    return 1.0 / (1.0 + jnp.exp2(x * NEG_LOG2_E))


def _dot(a, b):
    return jnp.dot(a, b, preferred_element_type=F32)


def _bf16_as_words(x):
    return pltpu.bitcast(x, jnp.uint32)


def _words_as_bf16(w):
    return pltpu.bitcast(w, BF16)


def _params(n_axes):
    return pltpu.CompilerParams(dimension_semantics=("arbitrary",) * n_axes,
                                vmem_limit_bytes=VMEM_LIMIT_BYTES)


def _dwconv_tile_work(hist_ref, cur_ref, w_ref, emit, unaligned):
    n_lt, rows, _ = cur_ref.shape
    tm = rows - HALO
    taps = w_ref.shape[0]

    def place_history(lt):
        cur_ref[lt, 0:HALO, :] = hist_ref[:, lt * LANES:(lt + 1) * LANES]

    def chunk(r0, lt):
        lanes = slice(lt * LANES, (lt + 1) * LANES)
        acc = None
        for k in range(taps):
            start = HALO + r0 - (taps - 1 - k)
            if start % SUBLANES:
                window = cur_ref[lt, pl.ds(unaligned + start, CONV_ROWS), :]
            else:
                window = cur_ref[lt, start:start + CONV_ROWS, :]
            term = window * w_ref[k:k + 1, lanes]
            acc = term if acc is None else acc + term
        emit(slice(r0, r0 + CONV_ROWS), lanes, acc)

    def refresh_history(lt):
        hist_ref[:, lt * LANES:(lt + 1) * LANES] = cur_ref[lt, tm:tm + HALO, :]

    tiles = range(n_lt)
    return ([functools.partial(place_history, lt) for lt in tiles]
            + [functools.partial(chunk, r0, lt) for r0 in range(0, tm, CONV_ROWS) for lt in tiles]
            + [functools.partial(refresh_history, lt) for lt in tiles])


def _interleave(major, minor):
    done = 0
    for k, item in enumerate(major):
        item()
        upto = len(minor) * (k + 1) // len(major)
        for m in minor[done:upto]:
            m()
        done = upto


def _in_proj_kernel(x_ref, g_ref, wah_ref, wab_ref, wac_ref, wgv_ref, wgg_ref, wgt_ref,
                    bv_ref, bg_ref, caw_ref, dww_ref, dwb_ref, *rest, nj, n_later):
    later_f32, rest = rest[:n_later], rest[n_later:]
    (ya_ref, c_ref, gt_ref), rest = rest[:3], rest[3:]
    later_bf16, rest = rest[:n_later], rest[n_later:]
    n_ref, hca_ref, hu_ref, ca0_ref, ca1_ref, u0_ref, u1_ref, ab0_ref, ab1_ref = rest

    def round_later(src_ref, dst_ref):
        dst_ref[...] = _bf16_as_words(src_ref[...].astype(BF16))

    t = pl.program_id(0)
    unaligned = jnp.minimum(t, 0)
    last = pl.num_programs(0) - 2
    j = jnp.minimum(t, last) % nj
    jp = (t + nj - 1) % nj

    @pl.when(t == 0)
    def _():
        for ref in (hca_ref, hu_ref, ca1_ref, u1_ref, ab1_ref):
            ref[...] = jnp.zeros(ref.shape, ref.dtype)

    @pl.when(j == 0)
    def _():
        n_ref[...] = _rmsnorm_rows(x_ref[...], g_ref[...]).astype(BF16)

    def step(ca_w, u_w, ab_w, ca_r, u_r, ab_r):
        def emit_c(rows, lanes, acc):
            c_ref[rows, lanes] = acc + dwb_ref[:, lanes]

        def emit_ya(rows, lanes, acc):
            ya_ref[rows, lanes] = (ab_r[rows, lanes].astype(F32) * acc).astype(BF16)

        conv_b = _dwconv_tile_work(hu_ref.at[jp], u_r, dww_ref, emit_c, unaligned)
        conv_a = _dwconv_tile_work(hca_ref.at[jp], ca_r, caw_ref, emit_ya, unaligned)
        convs = [w for pair in zip(conv_b, conv_a) for w in pair]
        convs += [functools.partial(round_later, src, dst) for src, dst in zip(later_f32, later_bf16)]

        def put(slab_ref, val):
            for lt in range(slab_ref.shape[0]):
                slab_ref[lt, HALO:, :] = val[:, lt * LANES:(lt + 1) * LANES]

        def get(slab_ref):
            return jnp.concatenate([slab_ref[lt, HALO:, :] for lt in range(slab_ref.shape[0])], axis=1)

        def glu_value():
            put(u_w, _dot(n_ref[...], wgv_ref[...]) + bv_ref[...])

        def glu_gate():
            put(u_w, get(u_w) * _sigmoid(_dot(n_ref[...], wgg_ref[...]) + bg_ref[...]))

        def conv_a_value():
            put(ca_w, _dot(n_ref[...], wah_ref[...]))

        def conv_a_in_gate():
            put(ca_w, get(ca_w) * _dot(n_ref[...], wac_ref[...]))

        def conv_a_out_gate():
            ab_w[...] = _dot(n_ref[...], wab_ref[...]).astype(BF16)

        def merge_gates(c0):
            cols = slice(c0, c0 + tn)
            gt_ref[:, cols] = _sigmoid(_dot(n_ref[...], wgt_ref[:, cols])).astype(BF16)

        tn = ab_w.shape[1]
        projections = [glu_value, glu_gate, conv_a_value, conv_a_in_gate, conv_a_out_gate] + [
            functools.partial(merge_gates, c0) for c0 in range(0, gt_ref.shape[1], tn)]
        _interleave(projections, convs)

    @pl.when(t % 2 == 0)
    def _():
        step(ca0_ref, u0_ref, ab0_ref, ca1_ref, u1_ref, ab1_ref)

    @pl.when(t % 2 == 1)
    def _():
        step(ca1_ref, u1_ref, ab1_ref, ca0_ref, u0_ref, ab0_ref)


BF16_ROWS = 16


def _row_block_spec(w, n_steps, cur):
    rows = w.shape[0]
    for repeat in (1, 2, 4):
        n_blocks = n_steps // repeat
        if rows % n_blocks == 0 and (rows // n_blocks) % BF16_ROWS == 0:
            return pl.BlockSpec((rows // n_blocks, w.shape[1]), lambda t, r=repeat: (cur(t) // r, 0))
    raise ValueError(f"cannot split {rows} rows over {n_steps} steps")


def _in_proj(x, g_mix, w_in, b_glu, conv_a_w, conf_dw_w, conf_dw_b, later_weights, *, tm, width, d_model):
    s = x.shape[0]
    nj = width // MXU_COLS
    tn = width // nj
    tg = 2 * d_model // nj
    gate_col0 = 5 * width
    n_steps = (s // tm) * nj
    assert n_steps % 2 == 0

    cur = lambda t: jnp.minimum(t, n_steps - 1)
    prev = lambda t: jnp.maximum(t - 1, 0)

    def wspec(col0):
        return pl.BlockSpec((d_model, tn), lambda t, c=col0 // tn: (0, c + cur(t) % nj))

    prev_col = lambda t: (0, prev(t) % nj)
    prev_tile = lambda t: (prev(t) // nj, prev(t) % nj)
    later_specs = [_row_block_spec(w, n_steps, cur) for w in later_weights]
    slab = pltpu.VMEM((tn // LANES, HALO + tm, LANES), F32)
    word_specs = [pl.BlockSpec((sp.block_shape[0] // 2, sp.block_shape[1]), sp.index_map) for sp in later_specs]
    outs = pl.pallas_call(
        functools.partial(_in_proj_kernel, nj=nj, n_later=len(later_weights)),
        grid=(n_steps + 1,),
        in_specs=[
            pl.BlockSpec((tm, d_model), lambda t: (cur(t) // nj, 0)),
            pl.BlockSpec((1, d_model), lambda t: (0, 0)),
            wspec(0), wspec(width), wspec(2 * width), wspec(3 * width), wspec(4 * width),
            pl.BlockSpec((d_model, tg), lambda t, c=gate_col0 // tg: (0, c + cur(t) % nj)),
            pl.BlockSpec((1, tn), lambda t: (0, cur(t) % nj)),
            pl.BlockSpec((1, tn), lambda t, c=width // tn: (0, c + cur(t) % nj)),
            pl.BlockSpec((conv_a_w.shape[0], tn), prev_col),
            pl.BlockSpec((conf_dw_w.shape[0], tn), prev_col),
            pl.BlockSpec((1, tn), prev_col),
        ] + later_specs,
        out_specs=[
            pl.BlockSpec((tm, tn), prev_tile),
            pl.BlockSpec((tm, tn), prev_tile),
            pl.BlockSpec((tm, tg), lambda t: (cur(t) // nj, cur(t) % nj)),
        ] + word_specs,
        out_shape=[
            jax.ShapeDtypeStruct((s, width), BF16),
            jax.ShapeDtypeStruct((s, width), F32),
            jax.ShapeDtypeStruct((s, 2 * d_model), BF16),
        ] + [jax.ShapeDtypeStruct((w.shape[0] // 2, w.shape[1]), jnp.uint32) for w in later_weights],
        scratch_shapes=[
            pltpu.VMEM((tm, d_model), BF16),
            pltpu.VMEM((nj, HALO, tn), F32),
            pltpu.VMEM((nj, HALO, tn), F32),
            slab, slab,
            slab, slab,
            pltpu.VMEM((tm, tn), BF16), pltpu.VMEM((tm, tn), BF16),
        ],
        compiler_params=_params(1),
        name="in_proj",
    )(x, g_mix, w_in, w_in, w_in, w_in, w_in, w_in, b_glu, b_glu, conv_a_w, conf_dw_w, conf_dw_b,
      *later_weights)
    return outs[:3], outs[3:]


LN_ROWS = 64


def _mixers_kernel(x_ref, ya_ref, c_ref, gt_ref, lng_ref, lnb_ref,
                   woa_ref, wpw_ref, bpw_ref, wo_ref, h1_ref, v_ref, m_ref, *, tn):
    tm, d_model = h1_ref.shape
    col_blocks = [slice(c0, c0 + tn) for c0 in range(0, d_model, tn)]

    def layernorm_swish(r0):
        rows = slice(r0, r0 + LN_ROWS)
        c = c_ref[rows, :]
        mu = jnp.mean(c, axis=-1, keepdims=True)
        xc = c - mu
        var = jnp.mean(xc * xc, axis=-1, keepdims=True)
        y = xc * lax.rsqrt(var + LN_EPS) * lng_ref[...] + lnb_ref[...]
        v_ref[rows, :] = (y * _sigmoid(y)).astype(BF16)

    def branch_a(cols):
        h1_ref[:, cols] = gt_ref[:, cols].astype(F32) * _dot(ya_ref[...], _words_as_bf16(woa_ref[:, cols]))

    def merge(cols):
        gate_b = gt_ref[:, d_model + cols.start:d_model + cols.stop].astype(F32)
        yb = _dot(v_ref[...], _words_as_bf16(wpw_ref[:, cols])) + bpw_ref[:, cols]
        m_ref[:, cols] = (h1_ref[:, cols] + gate_b * yb).astype(BF16)

    def out_proj(cols):
        h1_ref[:, cols] = x_ref[:, cols] + _dot(m_ref[...], _words_as_bf16(wo_ref[:, cols]))

    _interleave([functools.partial(branch_a, cols) for cols in col_blocks],
                [functools.partial(layernorm_swish, r0) for r0 in range(0, tm, LN_ROWS)])
    for cols in col_blocks:
        merge(cols)
    for cols in col_blocks:
        out_proj(cols)


def _mixers(x, ya, c, gates, ln_g, ln_b, w_out_a, w_pw_b, b_pw_b, w_o, *, tm, tn, width, d_model):
    s = x.shape[0]
    row = lambda i: (i, 0)
    fixed = lambda i: (0, 0)
    resident = dict(pipeline_mode=pl.Buffered(1))
    return pl.pallas_call(
        functools.partial(_mixers_kernel, tn=tn),
        grid=(s // tm,),
        in_specs=[
            pl.BlockSpec((tm, d_model), row),
            pl.BlockSpec((tm, width), row),
            pl.BlockSpec((tm, width), row),
            pl.BlockSpec((tm, 2 * d_model), row),
            pl.BlockSpec((1, width), fixed),
            pl.BlockSpec((1, width), fixed),
            pl.BlockSpec((width // 2, d_model), fixed, **resident),
            pl.BlockSpec((width // 2, d_model), fixed, **resident),
            pl.BlockSpec((1, d_model), fixed),
            pl.BlockSpec((d_model // 2, d_model), fixed, **resident),
        ],
        out_specs=pl.BlockSpec((tm, d_model), row),
        out_shape=jax.ShapeDtypeStruct((s, d_model), F32),
        scratch_shapes=[
            pltpu.VMEM((tm, width), BF16),
            pltpu.VMEM((tm, d_model), BF16),
        ],
        compiler_params=_params(1),
        name="mixers",
    )(x, ya, c, gates, ln_g, ln_b, w_out_a, w_pw_b, b_pw_b, w_o)


def _ffn_kernel(h_ref, g_ref, wg_ref, wu_ref, wd_ref, o_ref, n_ref, f_ref):
    @pl.when(pl.program_id(1) == 0)
    def _():
        h = h_ref[...]
        n_ref[...] = _rmsnorm_rows(h, g_ref[...]).astype(BF16)
        o_ref[...] = h

    tf = f_ref.shape[1]
    part = max(tf // 2, MXU_COLS)
    for c0 in range(0, tf, part):
        cols = slice(c0, c0 + part)
        gate = _dot(n_ref[...], _words_as_bf16(wg_ref[:, cols]))
        up = _dot(n_ref[...], _words_as_bf16(wu_ref[:, cols]))
        f_ref[:, cols] = (gate * _sigmoid(gate) * up).astype(BF16)
    for c0 in range(0, o_ref.shape[1], 2 * part):
        cols = slice(c0, c0 + 2 * part)
        o_ref[:, cols] += _dot(f_ref[...], _words_as_bf16(wd_ref[:, cols]))


def _ffn(h, g_ffn, w_gate, w_up, w_down, *, tm, tf):
    s, d_model = h.shape
    d_ff = w_gate.shape[1]
    return pl.pallas_call(
        _ffn_kernel,
        grid=(s // tm, d_ff // tf),
        in_specs=[
            pl.BlockSpec((tm, d_model), lambda i, k: (i, 0)),
            pl.BlockSpec((1, d_model), lambda i, k: (0, 0)),
            pl.BlockSpec((d_model // 2, tf), lambda i, k: (0, k)),
            pl.BlockSpec((d_model // 2, tf), lambda i, k: (0, k)),
            pl.BlockSpec((tf // 2, d_model), lambda i, k: (k, 0)),
        ],
        out_specs=pl.BlockSpec((tm, d_model), lambda i, k: (i, 0)),
        out_shape=jax.ShapeDtypeStruct((s, d_model), F32),
        scratch_shapes=[pltpu.VMEM((tm, d_model), BF16), pltpu.VMEM((tm, tf), BF16)],
        compiler_params=_params(2),
        name="ffn",
    )(h, g_ffn, w_gate, w_up, w_down)


def _ple_kernel(h_ref, p_ref, g_ref, wpg_ref, wpp_ref, gf_ref, o_ref, n_ref, *, tn):
    d_model = h_ref.shape[1]
    n_ref[...] = _rmsnorm_rows(h_ref[...], g_ref[...]).astype(BF16)
    pb = p_ref[...].astype(BF16)
    ssq = None
    for c0 in range(0, d_model, tn):
        cols = slice(c0, c0 + tn)
        gate = _sigmoid(_dot(n_ref[...], _words_as_bf16(wpg_ref[:, cols])))
        h3 = h_ref[:, cols] + gate * _dot(pb, _words_as_bf16(wpp_ref[:, cols]))
        o_ref[:, cols] = h3
        part = jnp.sum(h3 * h3, axis=-1, keepdims=True)
        ssq = part if ssq is None else ssq + part
    scale = lax.rsqrt(ssq / d_model + EPS)
    for c0 in range(0, d_model, tn):
        cols = slice(c0, c0 + tn)
        o_ref[:, cols] = o_ref[:, cols] * scale * gf_ref[:, cols]


def _ple(h, p, g_ple, w_ple_gate, w_ple_proj, g_final, *, tm, tn):
    s, d_model = h.shape
    ple_dim = p.shape[1]
    fixed = lambda i: (0, 0)
    resident = dict(pipeline_mode=pl.Buffered(1))
    return pl.pallas_call(
        functools.partial(_ple_kernel, tn=tn),
        grid=(s // tm,),
        in_specs=[
            pl.BlockSpec((tm, d_model), lambda i: (i, 0)),
            pl.BlockSpec((tm, ple_dim), lambda i: (i, 0)),
            pl.BlockSpec((1, d_model), fixed),
            pl.BlockSpec((d_model // 2, d_model), fixed, **resident),
            pl.BlockSpec((ple_dim // 2, d_model), fixed, **resident),
            pl.BlockSpec((1, d_model), fixed),
        ],
        out_specs=pl.BlockSpec((tm, d_model), lambda i: (i, 0)),
        out_shape=jax.ShapeDtypeStruct((s, d_model), F32),
        scratch_shapes=[pltpu.VMEM((tm, d_model), BF16)],
        compiler_params=_params(1),
        name="ple_final",
    )(h, p, g_ple, w_ple_gate, w_ple_proj, g_final)


def kernel(x, p, g_mix, w_in, conv_a_w, w_out_a, b_glu, conf_dw_w, conf_dw_b, conf_ln_g, conf_ln_b,
           w_pw_b, b_pw_b, w_o, g_ffn, w_gate, w_up, w_down, g_ple, w_ple_gate, w_ple_proj, g_final):
    batch, seq, d_model = x.shape
    depth = w_in.shape[0]
    width = conv_a_w.shape[2]
    assert batch == 1 and depth == 1 and conf_dw_w.shape[2] == width
    assert conv_a_w.shape[1] - 1 <= HALO and conf_dw_w.shape[1] - 1 <= HALO

    row = lambda a: a.reshape(1, -1)
    h = x.reshape(seq, d_model)
    later = (w_out_a[0], w_pw_b[0], w_o[0], w_gate[0], w_up[0], w_down[0], w_ple_gate[0], w_ple_proj[0])
    (ya, c, gates), (woa, wpw, wo, wg, wu, wd, wpg, wpp) = _in_proj(
        h, row(g_mix[0]), w_in[0].astype(BF16), row(b_glu[0]),
        conv_a_w[0], conf_dw_w[0], row(conf_dw_b[0]), later,
        tm=TILES.in_proj_tm, width=width, d_model=d_model)
    h = _mixers(h, ya, c, gates, row(conf_ln_g[0]), row(conf_ln_b[0]), woa, wpw, row(b_pw_b[0]), wo,
                tm=TILES.mixers_tm, tn=TILES.mixers_tn, width=width, d_model=d_model)
    h = _ffn(h, row(g_ffn[0]), wg, wu, wd, tm=TILES.ffn_tm, tf=TILES.ffn_tf)
    h = _ple(h, p[0, 0], row(g_ple[0]), wpg, wpp, row(g_final), tm=TILES.ple_tm, tn=TILES.ple_tn)
    return h.reshape(batch, seq, d_model)
```
